```python
import jax, jax.numpy as jnp
from jax import lax
import numpy as np

D_MODEL = 2048
BATCH = 4
SEQ = 2048
DEPTH = 2

N_B = DEPTH // 2
N_A = DEPTH - N_B
N_DENSE = (DEPTH + 1) // 2
N_MOE = DEPTH // 2
HEAD_DIM = 128
N_HEADS = D_MODEL // HEAD_DIM
CONV_WIDTH = 3
D_FF = 5632
N_EXPERTS = 8
TOP_K = 2
D_FF_EXPERT = 7168
PLE_DIM = 256
Q_BLOCK = 128
RMS_EPS = 1e-6
NEG_BIG = -1e30

kernel_name = "yoco_shortconv_fox_moe_ple"


def rms_norm(x, g):
    xf = x.astype(jnp.float32)
    y = xf * lax.rsqrt(jnp.mean(xf * xf, axis=-1, keepdims=True) + RMS_EPS) * g.astype(jnp.float32)
    return y.astype(x.dtype)


def short_conv_mixer(xn, w_in, w_dw, w_out):
    proj = xn @ w_in
    b_gate, c_gate, x_in = jnp.split(proj, 3, axis=-1)
    u = c_gate * x_in
    kern = w_dw[:, None, :].astype(u.dtype)
    conv = lax.conv_general_dilated(
        u, kern, window_strides=(1,), padding=[(CONV_WIDTH - 1, 0)],
        dimension_numbers=("NWC", "WIO", "NWC"), feature_group_count=D_MODEL)
    return (b_gate * conv) @ w_out


def shared_kv_side(h, kv_norm, w_kvf, b_f, k_norm):
    bsz, seq, _ = h.shape
    xs = rms_norm(h, kv_norm)
    kvf = xs @ w_kvf
    k = kvf[..., :D_MODEL].reshape(bsz, seq, N_HEADS, HEAD_DIM)
    v = kvf[..., D_MODEL:2 * D_MODEL].reshape(bsz, seq, N_HEADS, HEAD_DIM)
    f_logit = (kvf[..., 2 * D_MODEL:] + b_f).astype(jnp.float32)
    k = rms_norm(k, k_norm)
    c = jnp.cumsum(jax.nn.log_sigmoid(f_logit), axis=1)
    return (jnp.transpose(k, (0, 2, 1, 3)), jnp.transpose(v, (0, 2, 1, 3)),
            jnp.transpose(c, (0, 2, 1)))


def forgetting_attention(xn, k, v, c, w_q, q_norm, w_o):
    bsz, seq, _ = xn.shape
    n_blk = seq // Q_BLOCK
    q = rms_norm((xn @ w_q).reshape(bsz, seq, N_HEADS, HEAD_DIM), q_norm)
    q = jnp.transpose(q, (0, 2, 1, 3))
    q_blocks = jnp.moveaxis(q.reshape(bsz, N_HEADS, n_blk, Q_BLOCK, HEAD_DIM), 2, 0)
    c_blocks = jnp.moveaxis(c.reshape(bsz, N_HEADS, n_blk, Q_BLOCK), 2, 0)
    scale = HEAD_DIM ** -0.5
    k_pos = jnp.arange(seq)

    def one_block(args):
        qb, cqb, blk = args
        s = jnp.einsum("bhqd,bhkd->bhqk", qb, k).astype(jnp.float32) * scale
        s = s + cqb[..., :, None] - c[..., None, :]
        q_pos = blk * Q_BLOCK + jnp.arange(Q_BLOCK)
        s = jnp.where(k_pos[None, :] <= q_pos[:, None], s, NEG_BIG)
        a = jax.nn.softmax(s, axis=-1).astype(v.dtype)
        return jnp.einsum("bhqk,bhkd->bhqd", a, v)

    out = lax.map(one_block, (q_blocks, c_blocks, jnp.arange(n_blk)))
    out = jnp.moveaxis(out, 0, 2).reshape(bsz, N_HEADS, seq, HEAD_DIM)
    out = jnp.transpose(out, (0, 2, 1, 3)).reshape(bsz, seq, D_MODEL)
    return out @ w_o


def swiglu(xn, w_gu, w_down):
    g, u = jnp.split(xn @ w_gu, 2, axis=-1)
    return (jax.nn.silu(g) * u) @ w_down


def moe_swiglu(xn, w_router, w_gu, w_down):
    bsz, seq, d = xn.shape
    xf = xn.reshape(bsz * seq, d)
    logits = (xf @ w_router).astype(jnp.float32)
    top_v, top_i = lax.top_k(logits, TOP_K)
    top_w = jax.nn.softmax(top_v, axis=-1)
    gates = jnp.sum(jax.nn.one_hot(top_i, N_EXPERTS, dtype=jnp.float32) * top_w[..., None], axis=1)
    gates = gates.astype(xn.dtype)
    y = jnp.zeros_like(xf)
    for e in range(N_EXPERTS):
        y = y + gates[:, e:e + 1] * swiglu(xf, w_gu[e], w_down[e])
    return y.reshape(bsz, seq, d)


def setup_inputs(seed: int = 0) -> dict:
    key = jax.random.key(seed)
    ks = jax.random.split(key, 24)
    f32 = jnp.float32

    def nrm(k, shape, fan_in):
        return jax.random.normal(k, shape, f32) * (fan_in ** -0.5)

    def gain(k, shape):
        return 1.0 + 0.05 * jax.random.normal(k, shape, f32)

    D = D_MODEL
    return {
        "x": jax.random.normal(ks[0], (BATCH, SEQ, D), f32),
        "p": jax.random.normal(ks[1], (DEPTH, BATCH, SEQ, PLE_DIM), f32),
        "norm_mix": gain(ks[2], (DEPTH, D)),
        "norm_ffn": gain(ks[3], (DEPTH, D)),
        "norm_ple": gain(ks[4], (DEPTH, D)),
        "conv_in_w": nrm(ks[5], (N_A, D, 3 * D), D),
        "conv_dw": nrm(ks[6], (N_A, CONV_WIDTH, D), CONV_WIDTH),
        "conv_out_w": nrm(ks[7], (N_A, D, D), D),
        "kv_norm": gain(ks[8], (D,)),
        "w_kvf": nrm(ks[9], (D, 2 * D + N_HEADS), D),
        "b_f": 3.0 + 0.1 * jax.random.normal(ks[10], (N_HEADS,), f32),
        "k_norm": gain(ks[11], (HEAD_DIM,)),
        "attn_q_w": nrm(ks[12], (N_B, D, D), D),
        "q_norm": gain(ks[13], (N_B, HEAD_DIM)),
        "attn_out_w": nrm(ks[14], (N_B, D, D), D),
        "ffn_gu": nrm(ks[15], (N_DENSE, D, 2 * D_FF), D),
        "ffn_down": nrm(ks[16], (N_DENSE, D_FF, D), D_FF),
        "router_w": nrm(ks[17], (N_MOE, D, N_EXPERTS), D),
        "moe_gu": nrm(ks[18], (N_MOE, N_EXPERTS, D, 2 * D_FF_EXPERT), D),
        "moe_down": nrm(ks[19], (N_MOE, N_EXPERTS, D_FF_EXPERT, D), D_FF_EXPERT),
        "ple_up": nrm(ks[20], (DEPTH, PLE_DIM, D), PLE_DIM),
        "ple_gate": nrm(ks[21], (DEPTH, D, D), D),
    }


def reference(x, p, norm_mix, norm_ffn, norm_ple, conv_in_w, conv_dw, conv_out_w,
              kv_norm, w_kvf, b_f, k_norm, attn_q_w, q_norm, attn_out_w,
              ffn_gu, ffn_down, router_w, moe_gu, moe_down, ple_up, ple_gate):
    h = x
    shared = None
    for i in range(DEPTH):
        xn = rms_norm(h, norm_mix[i])
        if i < N_A:
            h = h + short_conv_mixer(xn, conv_in_w[i], conv_dw[i], conv_out_w[i])
        else:
            if i == N_A:
                shared = shared_kv_side(h, kv_norm, w_kvf, b_f, k_norm)
            j = i - N_A
            k_s, v_s, c_s = shared
            h = h + forgetting_attention(xn, k_s, v_s, c_s, attn_q_w[j], q_norm[j], attn_out_w[j])
        xn = rms_norm(h, norm_ffn[i])
        if i % 2 == 0:
            h = h + swiglu(xn, ffn_gu[i // 2], ffn_down[i // 2])
        else:
            h = h + moe_swiglu(xn, router_w[i // 2], moe_gu[i // 2], moe_down[i // 2])
        gate = jax.nn.sigmoid((rms_norm(h, norm_ple[i]) @ ple_gate[i]).astype(jnp.float32))
        h = h + gate.astype(h.dtype) * (p[i] @ ple_up[i])
    return h
```

```python
import functools

import jax
import jax.numpy as jnp
from jax import lax
from jax.experimental import pallas as pl
from jax.experimental.pallas import tpu as pltpu

RMS_EPS = 1e-6
NEG_BIG = -1e30
HEAD_DIM = 128
N_EXPERTS = 8
TOP_K = 2
CONV_WIDTH = 3

V7X_LANES = 128
V7X_VMEM_LIMIT_BYTES = 56 * 1024 * 1024

_F32 = jnp.float32
_BF16 = jnp.bfloat16
_ARB2 = ("arbitrary", "arbitrary")


def _params(sem):
    return pltpu.CompilerParams(dimension_semantics=sem,
                                vmem_limit_bytes=V7X_VMEM_LIMIT_BYTES)


def _dot(a, b):
    return jnp.dot(a, b, preferred_element_type=_F32)


def _rms_unit(xf):
    ms = jnp.mean(xf * xf, axis=-1, keepdims=True)
    return xf * lax.rsqrt(ms + RMS_EPS)


def _sigmoid(x):
    return 1.0 / (1.0 + jnp.exp(-x))


def _split3(x):
    hi = x.astype(_BF16)
    r = x - hi.astype(_F32)
    mid = r.astype(_BF16)
    lo = (r - mid.astype(_F32)).astype(_BF16)
    return hi, mid, lo


def _dot_f32ish(x, w):
    x1, x2, _ = _split3(x)
    w1, w2, _ = _split3(w)
    return _dot(x1, w1) + (_dot(x1, w2) + _dot(x2, w1))


def _norm_kernel(x_ref, g_ref, *o_refs):
    y = _rms_unit(x_ref[...])
    for k, o_ref in enumerate(o_refs):
        o_ref[...] = (y * g_ref[k:k + 1, :]).astype(o_ref.dtype)


def _rms_norm(h, gains, tr=256):
    t, d = h.shape
    g = gains.shape[0]
    return pl.pallas_call(
        _norm_kernel,
        grid=(t // tr,),
        in_specs=[pl.BlockSpec((tr, d), lambda i: (i, 0)),
                  pl.BlockSpec((g, d), lambda i: (0, 0))],
        out_specs=[pl.BlockSpec((tr, d), lambda i: (i, 0))] * g,
        out_shape=[jax.ShapeDtypeStruct((t, d), _BF16)] * g,
        compiler_params=_params(("arbitrary",)),
        name="rms_norm",
    )(h, gains)


def _cast_on_first_row(w_refs, wb_refs):
    @pl.when(pl.program_id(1) == 0)
    def _():
        for w_ref, wb_ref in zip(w_refs, wb_refs):
            wb_ref[...] = w_ref[...].astype(_BF16)


def _wspec(k, tn, layer, off):
    return pl.BlockSpec((None, k, tn), lambda j, i: (layer, 0, j + off))


def _mm_res_kernel(a_ref, w_ref, r_ref, o_ref, wb_ref):
    _cast_on_first_row([w_ref], [wb_ref])
    o_ref[...] = r_ref[...] + _dot(a_ref[...], wb_ref[...])


def _mm_res(a, w, layer, res, tm=512, tn=512):
    t, k = a.shape
    n = w.shape[2]
    return pl.pallas_call(
        _mm_res_kernel,
        grid=(n // tn, t // tm),
        in_specs=[pl.BlockSpec((tm, k), lambda j, i: (i, 0)),
                  _wspec(k, tn, layer, 0),
                  pl.BlockSpec((tm, tn), lambda j, i: (i, j))],
        out_specs=pl.BlockSpec((tm, tn), lambda j, i: (i, j)),
        out_shape=jax.ShapeDtypeStruct((t, n), _F32),
        scratch_shapes=[pltpu.VMEM((k, tn), _BF16)],
        compiler_params=_params(_ARB2),
        name="mm_res",
    )(a, w, res)


def _proj_kernel(a_ref, w_ref, g_ref, o_ref, wb_ref, *, head_norm):
    _cast_on_first_row([w_ref], [wb_ref])
    acc = _dot(a_ref[...], wb_ref[...])
    if head_norm:
        g = g_ref[...]
        for hh in range(acc.shape[1] // HEAD_DIM):
            sl = slice(hh * HEAD_DIM, (hh + 1) * HEAD_DIM)
            o_ref[:, sl] = (_rms_unit(acc[:, sl]) * g).astype(o_ref.dtype)
    else:
        o_ref[...] = acc.astype(o_ref.dtype)


def _proj(a, w, layer, col_off, n, head_gain, tm=512, tn=512):
    t, k = a.shape
    head_norm = head_gain is not None
    g = (head_gain if head_norm else jnp.ones((HEAD_DIM,), _F32)).reshape(1, HEAD_DIM)
    return pl.pallas_call(
        functools.partial(_proj_kernel, head_norm=head_norm),
        grid=(n // tn, t // tm),
        in_specs=[pl.BlockSpec((tm, k), lambda j, i: (i, 0)),
                  _wspec(k, tn, layer, col_off // tn),
                  pl.BlockSpec((1, HEAD_DIM), lambda j, i: (0, 0))],
        out_specs=pl.BlockSpec((tm, tn), lambda j, i: (i, j)),
        out_shape=jax.ShapeDtypeStruct((t, n), _BF16),
        scratch_shapes=[pltpu.VMEM((k, tn), _BF16)],
        compiler_params=_params(_ARB2),
        name="proj_headnorm" if head_norm else "proj",
    )(a, w, g)


def _swiglu_kernel(a_ref, wg_ref, wu_ref, o_ref, wgb_ref, wub_ref):
    _cast_on_first_row([wg_ref, wu_ref], [wgb_ref, wub_ref])
    a = a_ref[...]
    g = _dot(a, wgb_ref[...])
    u = _dot(a, wub_ref[...])
    o_ref[...] = (g * _sigmoid(g) * u).astype(o_ref.dtype)


def _swiglu_up(a, w_gu, layer, tm=512, tn=512):
    t, k = a.shape
    f = w_gu.shape[2] // 2
    return pl.pallas_call(
        _swiglu_kernel,
        grid=(f // tn, t // tm),
        in_specs=[pl.BlockSpec((tm, k), lambda j, i: (i, 0)),
                  _wspec(k, tn, layer, 0),
                  _wspec(k, tn, layer, f // tn)],
        out_specs=pl.BlockSpec((tm, tn), lambda j, i: (i, j)),
        out_shape=jax.ShapeDtypeStruct((t, f), _BF16),
        scratch_shapes=[pltpu.VMEM((k, tn), _BF16)] * 2,
        compiler_params=_params(_ARB2),
        name="swiglu_up",
    )(a, w_gu, w_gu)


def _ple_kernel(xn_ref, p_ref, wg_ref, wu_ref, h_ref, o_ref, wgb_ref, wub_ref):
    _cast_on_first_row([wg_ref, wu_ref], [wgb_ref, wub_ref])
    gate = _sigmoid(_dot(xn_ref[...], wgb_ref[...]))
    up = _dot(p_ref[...].astype(_BF16), wub_ref[...])
    o_ref[...] = h_ref[...] + gate * up


def _ple(xn, p, w_gate, w_up, layer, h, tm=512, tn=512):
    t, d = h.shape
    kp = p.shape[2]
    return pl.pallas_call(
        _ple_kernel,
        grid=(d // tn, t // tm),
        in_specs=[pl.BlockSpec((tm, d), lambda j, i: (i, 0)),
                  pl.BlockSpec((None, tm, kp), lambda j, i: (layer, i, 0)),
                  _wspec(d, tn, layer, 0),
                  _wspec(kp, tn, layer, 0),
                  pl.BlockSpec((tm, tn), lambda j, i: (i, j))],
        out_specs=pl.BlockSpec((tm, tn), lambda j, i: (i, j)),
        out_shape=jax.ShapeDtypeStruct((t, d), _F32),
        scratch_shapes=[pltpu.VMEM((d, tn), _BF16), pltpu.VMEM((kp, tn), _BF16)],
        compiler_params=_params(_ARB2),
        name="ple",
    )(xn, p, w_gate, w_up, h)


def _convin_kernel(a_ref, wb_in, wc_in, wx_in, dw_ref, o_ref, wb_b, wc_b, wx_b, tail_ref,
                   *, tiles_per_seq):
    _cast_on_first_row([wb_in, wc_in, wx_in], [wb_b, wc_b, wx_b])
    i = pl.program_id(1)
    a = a_ref[...]
    b_gate = _dot(a, wb_b[...])
    u = _dot(a, wc_b[...]) * _dot(a, wx_b[...])
    tm = u.shape[0]

    @pl.when(i % tiles_per_seq == 0)
    def _():
        tail_ref[...] = jnp.zeros_like(tail_ref)

    tail = tail_ref[...]
    p1 = tail[7:8, :]
    p2 = tail[6:7, :]
    row = lax.broadcasted_iota(jnp.int32, u.shape, 0)
    u1 = jnp.where(row == 0, p1, pltpu.roll(u, 1, 0))
    u2 = jnp.where(row == 0, p2, jnp.where(row == 1, p1, pltpu.roll(u, 2, 0)))
    dw = dw_ref[...]
    conv = dw[0:1, :] * u2 + dw[1:2, :] * u1 + dw[2:3, :] * u
    o_ref[...] = (b_gate * conv).astype(o_ref.dtype)
    tail_ref[...] = u[tm - 8:tm, :]


def _conv_in(a, w_in, w_dw, layer, seq, tm=512, tn=512):
    t, k = a.shape
    d = w_in.shape[2] // 3
    nb = d // tn
    return pl.pallas_call(
        functools.partial(_convin_kernel, tiles_per_seq=seq // tm),
        grid=(nb, t // tm),
        in_specs=[pl.BlockSpec((tm, k), lambda j, i: (i, 0)),
                  _wspec(k, tn, layer, 0),
                  _wspec(k, tn, layer, nb),
                  _wspec(k, tn, layer, 2 * nb),
                  pl.BlockSpec((None, CONV_WIDTH, tn), lambda j, i: (layer, 0, j))],
        out_specs=pl.BlockSpec((tm, tn), lambda j, i: (i, j)),
        out_shape=jax.ShapeDtypeStruct((t, d), _BF16),
        scratch_shapes=[pltpu.VMEM((k, tn), _BF16)] * 3 + [pltpu.VMEM((8, tn), _F32)],
        compiler_params=_params(_ARB2),
        name="conv_in",
    )(a, w_in, w_in, w_in, w_dw)


def _fgate_kernel(h_ref, g_ref, w_ref, bf_ref, c_ref, carry_ref):
    @pl.when(pl.program_id(1) == 0)
    def _():
        carry_ref[...] = jnp.zeros_like(carry_ref)

    xs = _rms_unit(h_ref[...]) * g_ref[...]
    fl = _dot_f32ish(xs, w_ref[...]) + bf_ref[...]
    ls = jnp.minimum(fl, 0.0) - jnp.log1p(jnp.exp(-jnp.abs(fl)))
    ts = ls.shape[0]
    row = lax.broadcasted_iota(jnp.int32, (ts, ts), 0)
    col = lax.broadcasted_iota(jnp.int32, (ts, ts), 1)
    tri = jnp.where(col <= row, 1.0, 0.0).astype(_BF16)
    l1, l2, l3 = _split3(ls)
    cs = _dot(tri, l1) + (_dot(tri, l2) + _dot(tri, l3))
    c = cs + carry_ref[0:1, :]
    c_ref[...] = c
    carry_ref[0:1, :] = c[ts - 1:ts, :]


def _forget_cumsum(h, gain, w_f, b_f, batch, seq, ts=256):
    t, d = h.shape
    nh = w_f.shape[1]
    w_pad = jnp.zeros((d, V7X_LANES), _F32).at[:, :nh].set(w_f)
    b_pad = jnp.zeros((1, V7X_LANES), _F32).at[0, :nh].set(b_f)
    spb = seq // ts
    return pl.pallas_call(
        _fgate_kernel,
        grid=(batch, spb),
        in_specs=[pl.BlockSpec((ts, d), lambda b, s: (b * spb + s, 0)),
                  pl.BlockSpec((1, d), lambda b, s: (0, 0)),
                  pl.BlockSpec((d, V7X_LANES), lambda b, s: (0, 0)),
                  pl.BlockSpec((1, V7X_LANES), lambda b, s: (0, 0))],
        out_specs=pl.BlockSpec((ts, V7X_LANES), lambda b, s: (b * spb + s, 0)),
        out_shape=jax.ShapeDtypeStruct((t, V7X_LANES), _F32),
        scratch_shapes=[pltpu.VMEM((8, V7X_LANES), _F32)],
        compiler_params=_params(_ARB2),
        name="forget_cumsum",
    )(h, gain.reshape(1, d), w_pad, b_pad)


def _attn_kernel(q_ref, k_ref, v_ref, cq_ref, ck_ref, o_ref, *, tq, scale):
    qi = pl.program_id(2)
    q = q_ref[...]
    cq = cq_ref[...]

    def scores(kj):
        ks = pl.multiple_of(kj * tq, tq)
        k = k_ref[pl.ds(ks, tq), :]
        s = lax.dot_general(q, k, (((1,), (1,)), ((), ())), preferred_element_type=_F32) * scale
        return s + cq - ck_ref[:, pl.ds(ks, tq)], ks

    def update(s, ks, carry):
        m, l, acc = carry
        m_new = jnp.maximum(m, jnp.max(s, axis=-1, keepdims=True))
        alpha = jnp.exp(m - m_new)
        p = jnp.exp(s - m_new)
        l = alpha * l + jnp.sum(p, axis=-1, keepdims=True)
        acc = alpha * acc + _dot(p.astype(_BF16), v_ref[pl.ds(ks, tq), :])
        return m_new, l, acc

    def body(kj, carry):
        s, ks = scores(kj)
        return update(s, ks, carry)

    init = (jnp.full((tq, 1), NEG_BIG, _F32), jnp.zeros((tq, 1), _F32),
            jnp.zeros((tq, HEAD_DIM), _F32))
    carry = lax.fori_loop(0, qi, body, init)
    s, ks = scores(qi)
    row = lax.broadcasted_iota(jnp.int32, s.shape, 0)
    col = lax.broadcasted_iota(jnp.int32, s.shape, 1)
    s = jnp.where(col <= row, s, NEG_BIG)
    _, l, acc = update(s, ks, carry)
    o_ref[...] = (acc / l).astype(o_ref.dtype)


def _attention(q, k, v, c_col, c_row, tq=512):
    b, s, d = q.shape
    nh = d // HEAD_DIM
    return pl.pallas_call(
        functools.partial(_attn_kernel, tq=tq, scale=HEAD_DIM ** -0.5),
        grid=(b, nh, s // tq),
        in_specs=[pl.BlockSpec((None, tq, HEAD_DIM), lambda bi, hi, qi: (bi, qi, hi)),
                  pl.BlockSpec((None, s, HEAD_DIM), lambda bi, hi, qi: (bi, 0, hi)),
                  pl.BlockSpec((None, s, HEAD_DIM), lambda bi, hi, qi: (bi, 0, hi)),
                  pl.BlockSpec((None, None, tq, 1), lambda bi, hi, qi: (bi, hi, qi, 0)),
                  pl.BlockSpec((None, None, 1, s), lambda bi, hi, qi: (bi, hi, 0, 0))],
        out_specs=pl.BlockSpec((None, tq, HEAD_DIM), lambda bi, hi, qi: (bi, qi, hi)),
        out_shape=jax.ShapeDtypeStruct((b, s, d), _BF16),
        compiler_params=_params(("arbitrary",) * 3),
        name="fox_attention",
    )(q, k, v, c_col, c_row)


def _router_kernel(h_ref, g_ref, w_ref, idx_ref, wgt_ref):
    xn = _rms_unit(h_ref[...]) * g_ref[...]
    logits = _dot_f32ish(xn, w_ref[...])
    lane_i = lax.broadcasted_iota(jnp.int32, logits.shape, 1)
    lane = lane_i.astype(_F32)
    neg_inf = jnp.float32(-jnp.inf)
    no_lane = jnp.float32(V7X_LANES)
    lg = jnp.where(lane_i < N_EXPERTS, logits, neg_inf)
    m1 = jnp.max(lg, axis=-1, keepdims=True)
    i1 = jnp.min(jnp.where(lg == m1, lane, no_lane), axis=-1, keepdims=True)
    lg2 = jnp.where(lane == i1, neg_inf, lg)
    m2 = jnp.max(lg2, axis=-1, keepdims=True)
    i2 = jnp.min(jnp.where(lg2 == m2, lane, no_lane), axis=-1, keepdims=True)
    e2 = jnp.exp(m2 - m1)
    den = 1.0 + e2
    idx_ref[...] = jnp.where(lane_i == 0, i1, jnp.where(lane_i == 1, i2, 0.0)).astype(jnp.int32)
    wgt_ref[...] = jnp.where(lane_i == 0, 1.0 / den, jnp.where(lane_i == 1, e2 / den, 0.0))


def _router(h, gain, w_router, tr=256):
    t, d = h.shape
    ne = w_router.shape[1]
    w_pad = jnp.zeros((d, V7X_LANES), _F32).at[:, :ne].set(w_router)
    idx, wgt = pl.pallas_call(
        _router_kernel,
        grid=(t // tr,),
        in_specs=[pl.BlockSpec((tr, d), lambda i: (i, 0)),
                  pl.BlockSpec((1, d), lambda i: (0, 0)),
                  pl.BlockSpec((d, V7X_LANES), lambda i: (0, 0))],
        out_specs=[pl.BlockSpec((tr, V7X_LANES), lambda i: (i, 0))] * 2,
        out_shape=[jax.ShapeDtypeStruct((t, V7X_LANES), jnp.int32),
                   jax.ShapeDtypeStruct((t, V7X_LANES), _F32)],
        compiler_params=_params(("arbitrary",)),
        name="router_top2",
    )(h, gain.reshape(1, d), w_pad)
    return idx[:, :TOP_K], wgt[:, :TOP_K]


def _moe_plan(top_i, top_w, tm):
    t = top_i.shape[0]
    n_assign = t * TOP_K
    n_tiles = n_assign // tm + N_EXPERTS
    n_rows = n_tiles * tm
    e_flat = top_i.reshape(-1)
    onehot = (e_flat[:, None] == jnp.arange(N_EXPERTS, dtype=jnp.int32)[None, :]).astype(jnp.int32)
    csum = jnp.cumsum(onehot, axis=0)
    counts = csum[-1]
    rank = jnp.sum(onehot * csum, axis=1) - 1
    tiles_per = (counts + tm - 1) // tm
    tile_end = jnp.cumsum(tiles_per)
    tile_start = tile_end - tiles_per
    pos = (tile_start * tm)[e_flat] + rank
    n_valid = tile_end[-1]
    tile_ids = jnp.arange(n_tiles, dtype=jnp.int32)
    tile_e = jnp.sum((tile_ids[:, None] >= tile_end[None, :]).astype(jnp.int32), axis=1)
    tile_e = jnp.minimum(tile_e, N_EXPERTS - 1)
    tile_e = jnp.where(tile_ids < n_valid, tile_e, tile_e[n_valid - 1])
    first = ((tile_ids == tile_start[tile_e]) & (tile_ids < n_valid)).astype(jnp.int32)
    src = jnp.zeros((n_rows,), jnp.int32).at[pos].set(jnp.arange(n_assign, dtype=jnp.int32) // TOP_K)
    row_w = jnp.zeros((n_rows,), _F32).at[pos].set(top_w.reshape(-1))
    return dict(pos=pos.astype(jnp.int32), src=src, row_w=row_w.reshape(n_rows, 1),
                tile_e=tile_e.astype(jnp.int32), first=first,
                n_valid=n_valid.astype(jnp.int32).reshape(1), n_tiles=n_tiles, n_rows=n_rows)


def _row_copy(src_hbm, row, dst_ref, r, sem):
    return pltpu.make_async_copy(src_hbm.at[pl.ds(row, 1)], dst_ref.at[pl.ds(r, 1)], sem)


def _gather_norm_kernel(idx_ref, h_hbm, g_ref, o_ref, buf_ref, sem):
    n = buf_ref.shape[0]

    def issue(r, c):
        _row_copy(h_hbm, idx_ref[0, r], buf_ref, r, sem).start()
        return c

    def drain(r, c):
        _row_copy(h_hbm, 0, buf_ref, r, sem).wait()
        return c

    lax.fori_loop(0, n, issue, 0)
    lax.fori_loop(0, n, drain, 0)
    o_ref[...] = (_rms_unit(buf_ref[...]) * g_ref[...]).astype(o_ref.dtype)


def _gather_norm(h, gain, src, tr=256):
    t, d = h.shape
    n_rows = src.shape[0]
    nblk = n_rows // tr
    return pl.pallas_call(
        _gather_norm_kernel,
        grid=(nblk,),
        in_specs=[pl.BlockSpec((None, 1, tr), lambda i: (i, 0, 0), memory_space=pltpu.SMEM),
                  pl.BlockSpec(memory_space=pl.ANY),
                  pl.BlockSpec((1, d), lambda i: (0, 0))],
        out_specs=pl.BlockSpec((tr, d), lambda i: (i, 0)),
        out_shape=jax.ShapeDtypeStruct((n_rows, d), _BF16),
        scratch_shapes=[pltpu.VMEM((tr, d), _F32), pltpu.SemaphoreType.DMA(())],
        compiler_params=_params(("arbitrary",)),
        name="moe_gather_norm",
    )(src.reshape(nblk, 1, tr), h, gain.reshape(1, d))


def _gmm_cast(valid, first_ref, w_refs, wb_refs):
    i = pl.program_id(1)

    @pl.when(valid & (first_ref[i] == 1))
    def _():
        for w_ref, wb_ref in zip(w_refs, wb_refs):
            wb_ref[...] = w_ref[...].astype(_BF16)


def _gmm_up_kernel(te_ref, first_ref, nv_ref, x_ref, wg_ref, wu_ref, o_ref, wgb_ref, wub_ref):
    valid = pl.program_id(1) < nv_ref[0]
    _gmm_cast(valid, first_ref, [wg_ref, wu_ref], [wgb_ref, wub_ref])

    @pl.when(valid)
    def _():
        x = x_ref[...]
        g = _dot(x, wgb_ref[...])
        u = _dot(x, wub_ref[...])
        o_ref[...] = (g * _sigmoid(g) * u).astype(o_ref.dtype)

    @pl.when(jnp.logical_not(valid))
    def _():
        o_ref[...] = jnp.zeros_like(o_ref)


def _gmm_up(xs, w_gu, layer, tile_e, first, n_valid, tm, tn=512):
    n_rows, d = xs.shape
    f = w_gu.shape[3] // 2
    w = w_gu[layer] if w_gu.shape[0] > 1 else w_gu.reshape(w_gu.shape[1:])
    nfb = f // tn

    def row(j, i, te, fi, nv):
        return jnp.minimum(i, nv[0] - 1)

    grid_spec = pltpu.PrefetchScalarGridSpec(
        num_scalar_prefetch=3,
        grid=(nfb, n_rows // tm),
        in_specs=[pl.BlockSpec((tm, d), lambda j, i, te, fi, nv: (row(j, i, te, fi, nv), 0)),
                  pl.BlockSpec((None, d, tn), lambda j, i, te, fi, nv: (te[i], 0, j)),
                  pl.BlockSpec((None, d, tn), lambda j, i, te, fi, nv: (te[i], 0, j + nfb))],
        out_specs=pl.BlockSpec((tm, tn), lambda j, i, te, fi, nv: (i, j)),
        scratch_shapes=[pltpu.VMEM((d, tn), _BF16)] * 2,
    )
    return pl.pallas_call(
        _gmm_up_kernel,
        grid_spec=grid_spec,
        out_shape=jax.ShapeDtypeStruct((n_rows, f), _BF16),
        compiler_params=_params(_ARB2),
        name="moe_gmm_up",
    )(tile_e, first, n_valid, xs, w, w)


def _gmm_down_kernel(te_ref, first_ref, nv_ref, x_ref, w_ref, rw_ref, o_ref, wb_ref):
    valid = pl.program_id(1) < nv_ref[0]
    _gmm_cast(valid, first_ref, [w_ref], [wb_ref])

    @pl.when(valid)
    def _():
        o_ref[...] = rw_ref[...] * _dot(x_ref[...], wb_ref[...])

    @pl.when(jnp.logical_not(valid))
    def _():
        o_ref[...] = jnp.zeros_like(o_ref)


def _gmm_down(hm, w_down, layer, row_w, tile_e, first, n_valid, tm, tn=512):
    n_rows, f = hm.shape
    d = w_down.shape[3]
    w = w_down[layer] if w_down.shape[0] > 1 else w_down.reshape(w_down.shape[1:])

    def row(j, i, te, fi, nv):
        return jnp.minimum(i, nv[0] - 1)

    grid_spec = pltpu.PrefetchScalarGridSpec(
        num_scalar_prefetch=3,
        grid=(d // tn, n_rows // tm),
        in_specs=[pl.BlockSpec((tm, f), lambda j, i, te, fi, nv: (row(j, i, te, fi, nv), 0)),
                  pl.BlockSpec((None, f, tn), lambda j, i, te, fi, nv: (te[i], 0, j)),
                  pl.BlockSpec((tm, 1), lambda j, i, te, fi, nv: (row(j, i, te, fi, nv), 0))],
        out_specs=pl.BlockSpec((tm, tn), lambda j, i, te, fi, nv: (i, j)),
        scratch_shapes=[pltpu.VMEM((f, tn), _BF16)],
    )
    return pl.pallas_call(
        _gmm_down_kernel,
        grid_spec=grid_spec,
        out_shape=jax.ShapeDtypeStruct((n_rows, d), _F32),
        compiler_params=_params(_ARB2),
        name="moe_gmm_down",
    )(tile_e, first, n_valid, hm, w, row_w)


def _combine_kernel(p0_ref, p1_ref, o_hbm, h_ref, g_ref, hn_ref, xn_ref, b0_ref, b1_ref, sem):
    n = b0_ref.shape[0]

    def issue(r, c):
        _row_copy(o_hbm, p0_ref[0, r], b0_ref, r, sem).start()
        _row_copy(o_hbm, p1_ref[0, r], b1_ref, r, sem).start()
        return c

    def drain(r, c):
        _row_copy(o_hbm, 0, b0_ref, r, sem).wait()
        _row_copy(o_hbm, 0, b1_ref, r, sem).wait()
        return c

    lax.fori_loop(0, n, issue, 0)
    lax.fori_loop(0, n, drain, 0)
    hn = h_ref[...] + (b0_ref[...] + b1_ref[...])
    hn_ref[...] = hn
    xn_ref[...] = (_rms_unit(hn) * g_ref[...]).astype(xn_ref.dtype)


def _combine_norm(h, o, pos0, pos1, gain, tr=256):
    t, d = h.shape
    nblk = t // tr
    smem = lambda: pl.BlockSpec((None, 1, tr), lambda i: (i, 0, 0), memory_space=pltpu.SMEM)
    return pl.pallas_call(
        _combine_kernel,
        grid=(nblk,),
        in_specs=[smem(), smem(),
                  pl.BlockSpec(memory_space=pl.ANY),
                  pl.BlockSpec((tr, d), lambda i: (i, 0)),
                  pl.BlockSpec((1, d), lambda i: (0, 0))],
        out_specs=[pl.BlockSpec((tr, d), lambda i: (i, 0))] * 2,
        out_shape=[jax.ShapeDtypeStruct((t, d), _F32), jax.ShapeDtypeStruct((t, d), _BF16)],
        scratch_shapes=[pltpu.VMEM((tr, d), _F32), pltpu.VMEM((tr, d), _F32),
                        pltpu.SemaphoreType.DMA(())],
        compiler_params=_params(("arbitrary",)),
        name="moe_combine_norm",
    )(pos0.reshape(nblk, 1, tr), pos1.reshape(nblk, 1, tr), o, h, gain.reshape(1, d))


MOE_ROW_TILE = 512
MOE_DOWN_ROW_TILE = 256


def kernel(x, p, norm_mix, norm_ffn, norm_ple, conv_in_w, conv_dw, conv_out_w, kv_norm, w_kvf, b_f, k_norm, attn_q_w, q_norm, attn_out_w, ffn_gu, ffn_down, router_w, moe_gu, moe_down, ple_up, ple_gate):
    batch, seq, d = x.shape
    t = batch * seq
    depth = p.shape[0]
    n_a = conv_in_w.shape[0]
    nh = d // HEAD_DIM
    p2 = p.reshape(depth, t, p.shape[3])
    h = x.reshape(t, d)
    xn_ple = None
    shared = None
    for i in range(depth):
        if i < n_a:
            (xn,) = _rms_norm(h, norm_mix[i:i + 1])
            v = _conv_in(xn, conv_in_w, conv_dw, i, seq)
            h = _mm_res(v, conv_out_w, i, h)
        else:
            if i == n_a:
                xn, xs = _rms_norm(h, jnp.stack([norm_mix[i], kv_norm]))
                w_kvf3 = w_kvf.reshape(1, d, w_kvf.shape[1])
                k = _proj(xs, w_kvf3, 0, 0, d, k_norm)
                vv = _proj(xs, w_kvf3, 0, d, d, None)
                c = _forget_cumsum(h, kv_norm, w_kvf[:, 2 * d:], b_f, batch, seq)
                c = jnp.transpose(c[:, :nh].reshape(batch, seq, nh), (0, 2, 1))
                shared = (k.reshape(batch, seq, d), vv.reshape(batch, seq, d),
                          c.reshape(batch, nh, seq, 1), c.reshape(batch, nh, 1, seq))
            else:
                (xn,) = _rms_norm(h, norm_mix[i:i + 1])
            j = i - n_a
            q = _proj(xn, attn_q_w, j, 0, d, q_norm[j])
            att = _attention(q.reshape(batch, seq, d), *shared)
            h = _mm_res(att.reshape(t, d), attn_out_w, j, h)
        if i % 2 == 0:
            (xn,) = _rms_norm(h, norm_ffn[i:i + 1])
            hm = _swiglu_up(xn, ffn_gu, i // 2)
            h = _mm_res(hm, ffn_down, i // 2, h)
            (xn_ple,) = _rms_norm(h, norm_ple[i:i + 1])
        else:
            top_i, top_w = _router(h, norm_ffn[i], router_w[i // 2])
            plan = _moe_plan(top_i, top_w, MOE_ROW_TILE)
            xs_moe = _gather_norm(h, norm_ffn[i], plan["src"])
            hm = _gmm_up(xs_moe, moe_gu, i // 2, plan["tile_e"], plan["first"], plan["n_valid"],
                         MOE_ROW_TILE)
            sub = MOE_ROW_TILE // MOE_DOWN_ROW_TILE
            first_sub = (plan["first"][:, None]
                         * (jnp.arange(sub, dtype=jnp.int32) == 0)[None, :]).reshape(-1)
            o = _gmm_down(hm, moe_down, i // 2, plan["row_w"], jnp.repeat(plan["tile_e"], sub),
                          first_sub, plan["n_valid"] * sub, MOE_DOWN_ROW_TILE)
            pos = plan["pos"].reshape(t, TOP_K)
            h, xn_ple = _combine_norm(h, o, pos[:, 0], pos[:, 1], norm_ple[i])
        h = _ple(xn_ple, p2, ple_gate, ple_up, i, h)
    return h.reshape(batch, seq, d)
```
